```python
import math
import jax, jax.numpy as jnp
from jax import lax
import numpy as np

D_MODEL = 2048
BATCH = 1
SEQ = 8192
DEPTH = 2

D_SSM = D_MODEL // 2
SSM_GROUP = 16
N_SSM_GROUPS = D_SSM // SSM_GROUP
SSM_STATE = 64
N_DIR = 2
D_ATTN = D_MODEL // 2
N_HEADS = 8
HEAD_DIM = D_ATTN // (2 * N_HEADS)
ROT_DIM = HEAD_DIM // 4
ROPE_THETA = 500000.0
Q_BLOCK = 128
N_BRANCH = 2
IN_COLS = D_SSM + 3 * D_ATTN + N_BRANCH * D_MODEL
N_EXPERTS = 16
D_EXPERT = 1024
CAPACITY_FACTOR = 2
EPS = 1e-6

kernel_name = "hybrid_s5_diffattn_ec_moe_encoder"


def rms_norm(x, g):
    xf = x.astype(jnp.float32)
    y = xf * lax.rsqrt(jnp.mean(xf * xf, axis=-1, keepdims=True) + EPS) * g.astype(jnp.float32)
    return y.astype(x.dtype)


def lambda_init_for(layer_idx):
    return 0.8 - 0.6 * math.exp(-0.3 * layer_idx)


def rope_tables(seq_len):
    pos = jnp.arange(seq_len, dtype=jnp.float32)
    inv = jnp.power(ROPE_THETA, -jnp.arange(0, ROT_DIM, 2, dtype=jnp.float32) / ROT_DIM)
    ang = pos[:, None] * inv[None, :]
    return jnp.cos(ang), jnp.sin(ang)


def apply_partial_rope(x, cos, sin):
    half = ROT_DIM // 2
    c = cos[None, :, None, None, :].astype(x.dtype)
    s = sin[None, :, None, None, :].astype(x.dtype)
    x1 = x[..., :half]
    x2 = x[..., half:ROT_DIM]
    rot = jnp.concatenate([x1 * c - x2 * s, x2 * c + x1 * s], axis=-1)
    return jnp.concatenate([rot, x[..., ROT_DIM:]], axis=-1)


def _ssm_combine(e1, e2):
    a1, b1 = e1
    a2, b2 = e2
    return a1 * a2, a2 * b1 + b2


def s5_bidirectional(u, a_re, a_im, log_step, b_re, b_im, c_re, c_im, d_skip):
    B, L, _ = u.shape
    f32 = jnp.float32
    uf = u.astype(f32)
    ug = uf.reshape(B, L, N_SSM_GROUPS, SSM_GROUP).astype(jnp.complex64)
    lam = lax.complex(a_re.astype(f32), a_im.astype(f32))
    step = jnp.exp(log_step.astype(f32))[..., None]
    lam_bar = jnp.exp(lam * step)
    b_bar = ((lam_bar - 1.0) / lam)[..., None] * lax.complex(b_re.astype(f32), b_im.astype(f32))
    c = lax.complex(c_re.astype(f32), c_im.astype(f32))
    y = d_skip.astype(f32) * uf
    for direction in range(N_DIR):
        bu = jnp.einsum('blgh,gph->blgp', ug, b_bar[direction])
        a = jnp.broadcast_to(lam_bar[direction], bu.shape)
        _, states = lax.associative_scan(_ssm_combine, (a, bu), reverse=(direction == 1), axis=1)
        y = y + jnp.real(jnp.einsum('blgp,ghp->blgh', states, c[direction])).reshape(B, L, D_SSM)
    return y.astype(u.dtype)


def diff_attention(q, k, v, q_g, k_g, lq1, lk1, lq2, lk2, sub_g, lam_init, cos, sin):
    B, L, _ = q.shape
    q = q.reshape(B, L, N_HEADS, 2, HEAD_DIM)
    k = k.reshape(B, L, N_HEADS, 2, HEAD_DIM)
    v = v.reshape(B, L, N_HEADS, 2 * HEAD_DIM)
    q = apply_partial_rope(rms_norm(q, q_g), cos, sin) * (HEAD_DIM ** -0.5)
    k = apply_partial_rope(rms_norm(k, k_g), cos, sin)
    f32 = jnp.float32
    lam = (jnp.exp(jnp.sum(lq1.astype(f32) * lk1.astype(f32)))
           - jnp.exp(jnp.sum(lq2.astype(f32) * lk2.astype(f32))) + lam_init)
    nb = L // Q_BLOCK
    qb = q.reshape(B, nb, Q_BLOCK, N_HEADS, 2, HEAD_DIM).transpose(1, 0, 3, 4, 2, 5)
    kt = k.transpose(0, 2, 3, 1, 4)
    vt = v.transpose(0, 2, 1, 3)

    def block(q_blk):
        s = jnp.einsum('bhcqd,bhckd->bhcqk', q_blk, kt).astype(f32)
        p = jax.nn.softmax(s, axis=-1)
        a = p[:, :, 0] - lam * p[:, :, 1]
        return jnp.einsum('bhqk,bhkv->bhqv', a.astype(vt.dtype), vt)

    o = lax.map(block, qb)
    o = o.transpose(1, 0, 3, 2, 4).reshape(B, L, N_HEADS, 2 * HEAD_DIM)
    o = rms_norm(o, sub_g) * (1.0 - lam_init)
    return o.reshape(B, L, D_ATTN)


def expert_choice_moe(x, w_router, w_gate, w_up, w_down):
    B, T, D = x.shape
    cap = max(1, CAPACITY_FACTOR * T // N_EXPERTS)
    logits = jnp.einsum('btd,de->bte', x, w_router).astype(jnp.float32)
    aff = jax.nn.softmax(logits, axis=-1)
    gate, idx = lax.top_k(aff.transpose(0, 2, 1), cap)
    xe = jax.vmap(lambda xb, ib: xb[ib])(x, idx)
    h = jax.nn.silu(jnp.einsum('becd,edf->becf', xe, w_gate)) * jnp.einsum('becd,edf->becf', xe, w_up)
    ye = jnp.einsum('becf,efd->becd', h, w_down) * gate[..., None].astype(x.dtype)
    return jax.vmap(lambda yb, ib: jnp.zeros((T, D), yb.dtype).at[ib.reshape(-1)].add(yb.reshape(-1, D)))(ye, idx)


def setup_inputs(seed: int = 0) -> dict:
    key = jax.random.key(seed)
    ks = jax.random.split(key, 32)
    f32 = jnp.float32
    G, P, H = N_SSM_GROUPS, SSM_STATE, SSM_GROUP

    def nrm(k, shape, scale):
        return jax.random.normal(k, shape, f32) * scale

    def gain(k, shape):
        return 1.0 + 0.02 * jax.random.normal(k, shape, f32)

    n_idx = jnp.arange(P, dtype=f32)
    a_re = -0.5 + 0.01 * jax.random.normal(ks[3], (DEPTH, N_DIR, G, P), f32)
    a_im = math.pi * n_idx + 0.01 * jax.random.normal(ks[4], (DEPTH, N_DIR, G, P), f32)
    log_step = jax.random.uniform(ks[5], (DEPTH, N_DIR, G), f32, math.log(1e-3), math.log(1e-1))
    return {
        "x": jax.random.normal(ks[0], (BATCH, SEQ, D_MODEL), f32),
        "mix_norm_g": gain(ks[1], (DEPTH, D_MODEL)),
        "w_in": nrm(ks[2], (DEPTH, D_MODEL, IN_COLS), D_MODEL ** -0.5),
        "gate_b": nrm(ks[6], (DEPTH, N_BRANCH, D_MODEL), 0.01),
        "ssm_a_re": a_re,
        "ssm_a_im": a_im,
        "ssm_log_step": log_step,
        "ssm_b_re": nrm(ks[7], (DEPTH, N_DIR, G, P, H), (2 * H) ** -0.5),
        "ssm_b_im": nrm(ks[8], (DEPTH, N_DIR, G, P, H), (2 * H) ** -0.5),
        "ssm_c_re": nrm(ks[9], (DEPTH, N_DIR, G, H, P), P ** -0.5),
        "ssm_c_im": nrm(ks[10], (DEPTH, N_DIR, G, H, P), P ** -0.5),
        "ssm_d": nrm(ks[11], (DEPTH, D_SSM), 1.0),
        "w_glu": nrm(ks[12], (DEPTH, D_SSM, D_SSM), D_SSM ** -0.5),
        "q_norm_g": gain(ks[13], (DEPTH, HEAD_DIM)),
        "k_norm_g": gain(ks[14], (DEPTH, HEAD_DIM)),
        "lambda_q1": nrm(ks[15], (DEPTH, HEAD_DIM), 0.1),
        "lambda_k1": nrm(ks[16], (DEPTH, HEAD_DIM), 0.1),
        "lambda_q2": nrm(ks[17], (DEPTH, HEAD_DIM), 0.1),
        "lambda_k2": nrm(ks[18], (DEPTH, HEAD_DIM), 0.1),
        "subln_g": gain(ks[19], (DEPTH, 2 * HEAD_DIM)),
        "w_br_ssm": nrm(ks[20], (DEPTH, D_SSM, D_MODEL), D_SSM ** -0.5),
        "w_br_attn": nrm(ks[21], (DEPTH, D_ATTN, D_MODEL), D_ATTN ** -0.5),
        "w_out": nrm(ks[22], (DEPTH, D_MODEL, D_MODEL), D_MODEL ** -0.5),
        "ffn_norm_g": gain(ks[23], (DEPTH, D_MODEL)),
        "w_router": nrm(ks[24], (DEPTH, D_MODEL, N_EXPERTS), D_MODEL ** -0.5),
        "w_e_gate": nrm(ks[25], (DEPTH, N_EXPERTS, D_MODEL, D_EXPERT), D_MODEL ** -0.5),
        "w_e_up": nrm(ks[26], (DEPTH, N_EXPERTS, D_MODEL, D_EXPERT), D_MODEL ** -0.5),
        "w_e_down": nrm(ks[27], (DEPTH, N_EXPERTS, D_EXPERT, D_MODEL), D_EXPERT ** -0.5),
    }


def reference(x, mix_norm_g, w_in, gate_b, ssm_a_re, ssm_a_im, ssm_log_step, ssm_b_re, ssm_b_im,
              ssm_c_re, ssm_c_im, ssm_d, w_glu, q_norm_g, k_norm_g, lambda_q1, lambda_k1,
              lambda_q2, lambda_k2, subln_g, w_br_ssm, w_br_attn, w_out, ffn_norm_g, w_router,
              w_e_gate, w_e_up, w_e_down):
    B, L, _ = x.shape
    cos, sin = rope_tables(L)
    splits = [D_SSM, D_SSM + D_ATTN, D_SSM + 2 * D_ATTN, D_SSM + 3 * D_ATTN]
    for l in range(DEPTH):
        lam_init = lambda_init_for(l)
        h = rms_norm(x, mix_norm_g[l])
        proj = h @ w_in[l]
        u_ssm, q, k, v, gate_logits = jnp.split(proj, splits, axis=-1)
        y_s = s5_bidirectional(u_ssm, ssm_a_re[l], ssm_a_im[l], ssm_log_step[l], ssm_b_re[l],
                               ssm_b_im[l], ssm_c_re[l], ssm_c_im[l], ssm_d[l])
        y_s = jax.nn.gelu(y_s)
        y_s = y_s * jax.nn.sigmoid(y_s @ w_glu[l])
        y_a = diff_attention(q, k, v, q_norm_g[l], k_norm_g[l], lambda_q1[l], lambda_k1[l],
                             lambda_q2[l], lambda_k2[l], subln_g[l], lam_init, cos, sin)
        g = jax.nn.sigmoid((gate_logits.reshape(B, L, N_BRANCH, D_MODEL) + gate_b[l]).astype(jnp.float32)).astype(x.dtype)
        merged = g[:, :, 0] * (y_s @ w_br_ssm[l]) + g[:, :, 1] * (y_a @ w_br_attn[l])
        x = x + merged @ w_out[l]
        x = x + expert_choice_moe(rms_norm(x, ffn_norm_g[l]), w_router[l], w_e_gate[l],
                                  w_e_up[l], w_e_down[l])
    return x
```

```python
import functools
import math

import jax
import jax.numpy as jnp
from jax import lax
from jax.experimental import pallas as pl
from jax.experimental.pallas import tpu as pltpu

F32 = jnp.float32
BF16 = jnp.bfloat16

D_MODEL = 2048
SEQ = 8192
DEPTH = 2
D_SSM = 1024
SSM_GROUP = 16
N_SSM_GROUPS = 64
SSM_STATE = 64
D_ATTN = 1024
N_HEADS = 8
HEAD_DIM = 64
V_DIM = 2 * HEAD_DIM
ROT_DIM = 16
ROPE_THETA = 500000.0
IN_COLS = D_SSM + 3 * D_ATTN + 2 * D_MODEL
N_EXPERTS = 16
D_EXPERT = 1024
CAPACITY = 2 * SEQ // N_EXPERTS
EPS = 1e-6
LOG2E = 1.4426950408889634

LANES = 128
VMEM_LIMIT_BYTES = 56 * 1024 * 1024

SSM_CHUNK = 16
CHUNK_W = SSM_CHUNK * SSM_GROUP
N_CHUNKS = SEQ // SSM_CHUNK
S5_GROUPS_PER_STEP = 8
S5_PAIRS_PER_STEP = S5_GROUPS_PER_STEP // 2

TOK_BLOCKS = SEQ // LANES
SLOT_ROWS = CAPACITY // LANES + 1


def _params(*sem):
    return pltpu.CompilerParams(dimension_semantics=sem, vmem_limit_bytes=VMEM_LIMIT_BYTES)


def _sigmoid(x):
    return 1.0 / (1.0 + jnp.exp(-x))


def _dot(a, b):
    return jnp.dot(a, b, preferred_element_type=F32)


INPROJ_TM = 1024
INPROJ_TN = 512


def _inproj_kernel(x_ref, g_ref, w_ref, o_ref, xn_ref):
    @pl.when(pl.program_id(1) == 0)
    def _():
        x = x_ref[...]
        ms = jnp.mean(x * x, axis=-1, keepdims=True)
        xn_ref[...] = (x * lax.rsqrt(ms + EPS) * g_ref[...]).astype(BF16)

    o_ref[...] = _dot(xn_ref[...], w_ref[...].astype(BF16)).astype(o_ref.dtype)


def _inproj(x, g, w):
    tm, tn = INPROJ_TM, INPROJ_TN
    return pl.pallas_call(
        _inproj_kernel,
        grid=(SEQ // tm, IN_COLS // tn),
        in_specs=[
            pl.BlockSpec((tm, D_MODEL), lambda i, j: (i, 0)),
            pl.BlockSpec((1, D_MODEL), lambda i, j: (0, 0)),
            pl.BlockSpec((D_MODEL, tn), lambda i, j: (0, j)),
        ],
        out_specs=pl.BlockSpec((tm, tn), lambda i, j: (i, j)),
        out_shape=jax.ShapeDtypeStruct((SEQ, IN_COLS), BF16),
        scratch_shapes=[pltpu.VMEM((tm, D_MODEL), BF16)],
        compiler_params=_params("arbitrary", "arbitrary"),
        name="inproj",
    )(x, g.reshape(1, D_MODEL), w)


def _s5_matrices(a_re, a_im, log_step, b_re, b_im, c_re, c_im):
    T, H, P, G = SSM_CHUNK, SSM_GROUP, SSM_STATE, N_SSM_GROUPS
    lam = lax.complex(a_re.astype(F32), a_im.astype(F32))
    dt_lam = lam * jnp.exp(log_step.astype(F32))[..., None]
    lam_bar = jnp.exp(dt_lam)
    b_bar = ((lam_bar - 1.0) / lam)[..., None] * lax.complex(b_re.astype(F32), b_im.astype(F32))
    c = lax.complex(c_re.astype(F32), c_im.astype(F32))
    j = jnp.arange(T + 1, dtype=F32)
    pw = jnp.exp(dt_lam[:, :, None, :] * j[None, None, :, None])

    kern = jnp.real(jnp.einsum('dghp,dgjp,dgpk->dgjhk', c, pw[:, :, :T], b_bar))
    s = jnp.arange(T)[:, None]
    t = jnp.arange(T)[None, :]

    def toeplitz(k_d, lag):
        m = k_d[:, jnp.clip(lag, 0, T - 1)]
        m = jnp.where((lag >= 0)[None, :, :, None, None], m, 0.0)
        return m.transpose(0, 1, 4, 2, 3).reshape(G, T * H, T * H)

    m_mat = jnp.stack([toeplitz(kern[0], t - s), toeplitz(kern[1], s - t)])

    sidx = jnp.arange(T)
    w0 = pw[0][:, T - 1 - sidx][..., None] * b_bar[0][:, None]
    w1 = pw[1][:, sidx][..., None] * b_bar[1][:, None]
    w_st = jnp.stack([w0, w1]).transpose(0, 1, 2, 4, 3).reshape(2, G, T * H, P)
    v0 = c[0][:, None] * pw[0][:, 1 + sidx][:, :, None, :]
    v1 = c[1][:, None] * pw[1][:, T - sidx][:, :, None, :]
    v_st = jnp.stack([v0, v1]).reshape(2, G, T * H, P).transpose(0, 1, 3, 2)
    a_t = pw[:, :, T]

    def pair_rows(w):
        w = w.reshape(2, G // 2, 2, T * H, P)
        z = jnp.zeros_like(w[:, :, 0])
        top = jnp.concatenate([w[:, :, 0], z], axis=-1)
        bot = jnp.concatenate([z, w[:, :, 1]], axis=-1)
        return jnp.concatenate([top, bot], axis=-2)

    def pair_cols(v):
        v = v.reshape(2, G // 2, 2, P, T * H)
        z = jnp.zeros_like(v[:, :, 0])
        top = jnp.concatenate([v[:, :, 0], z], axis=-1)
        bot = jnp.concatenate([z, v[:, :, 1]], axis=-1)
        return jnp.concatenate([top, bot], axis=-2)

    a_pair = a_t.reshape(2, G // 2, 1, 2 * P)
    return dict(
        m=m_mat.astype(BF16),
        wre=pair_rows(jnp.real(w_st)).astype(BF16), wim=pair_rows(jnp.imag(w_st)).astype(BF16),
        vre=pair_cols(jnp.real(v_st)).astype(BF16), vim=pair_cols(-jnp.imag(v_st)).astype(BF16),
        are=jnp.real(a_pair).astype(F32), aim=jnp.imag(a_pair).astype(F32),
    )


def _s5_kernel(u_ref, m_ref, wre_ref, wim_ref, vre_ref, vim_ref, are_ref, aim_ref, y_ref,
               pre_re, pre_im, ein_re, ein_im):
    npair = S5_PAIRS_PER_STEP
    pw = 2 * CHUNK_W
    for d in range(2):
        for p in range(npair):
            up = u_ref[:, p * pw:(p + 1) * pw]
            pre_re[d, p] = _dot(up, wre_ref[d, p])
            pre_im[d, p] = _dot(up, wim_ref[d, p])

    def step(ci, carry):
        out = []
        for d in range(2):
            row = ci if d == 0 else N_CHUNKS - 1 - ci
            for p in range(npair):
                k = 2 * (d * npair + p)
                sr, si = carry[k], carry[k + 1]
                ein_re[d, p, pl.ds(row, 1), :] = sr
                ein_im[d, p, pl.ds(row, 1), :] = si
                ar = are_ref[d, p]
                ai = aim_ref[d, p]
                out.append(ar * sr - ai * si + pre_re[d, p, pl.ds(row, 1), :])
                out.append(ar * si + ai * sr + pre_im[d, p, pl.ds(row, 1), :])
        return tuple(out)

    zero = jnp.zeros((1, 2 * SSM_STATE), F32)
    lax.fori_loop(0, N_CHUNKS, step, (zero,) * (4 * npair), unroll=2)

    for p in range(npair):
        acc = jnp.zeros((N_CHUNKS, pw), F32)
        for d in range(2):
            acc = acc + _dot(ein_re[d, p].astype(BF16), vre_ref[d, p])
            acc = acc + _dot(ein_im[d, p].astype(BF16), vim_ref[d, p])
        for gi in range(2):
            g = 2 * p + gi
            ug = u_ref[:, g * CHUNK_W:(g + 1) * CHUNK_W]
            intra = _dot(ug, m_ref[0, g]) + _dot(ug, m_ref[1, g])
            y_ref[:, g * CHUNK_W:(g + 1) * CHUNK_W] = intra + acc[:, gi * CHUNK_W:(gi + 1) * CHUNK_W]


def _s5(u_flat, mats):
    gs, ps = S5_GROUPS_PER_STEP, S5_PAIRS_PER_STEP
    wblk = gs * CHUNK_W
    st = 2 * SSM_STATE
    return pl.pallas_call(
        _s5_kernel,
        grid=(N_SSM_GROUPS // gs,),
        in_specs=[
            pl.BlockSpec((N_CHUNKS, wblk), lambda i: (0, i)),
            pl.BlockSpec((2, gs, CHUNK_W, CHUNK_W), lambda i: (0, i, 0, 0)),
            pl.BlockSpec((2, ps, 2 * CHUNK_W, st), lambda i: (0, i, 0, 0)),
            pl.BlockSpec((2, ps, 2 * CHUNK_W, st), lambda i: (0, i, 0, 0)),
            pl.BlockSpec((2, ps, st, 2 * CHUNK_W), lambda i: (0, i, 0, 0)),
            pl.BlockSpec((2, ps, st, 2 * CHUNK_W), lambda i: (0, i, 0, 0)),
            pl.BlockSpec((2, ps, 1, st), lambda i: (0, i, 0, 0)),
            pl.BlockSpec((2, ps, 1, st), lambda i: (0, i, 0, 0)),
        ],
        out_specs=pl.BlockSpec((N_CHUNKS, wblk), lambda i: (0, i)),
        out_shape=jax.ShapeDtypeStruct((N_CHUNKS, N_SSM_GROUPS * CHUNK_W), F32),
        scratch_shapes=[pltpu.VMEM((2, ps, N_CHUNKS, st), F32)] * 4,
        compiler_params=_params("arbitrary"),
        name="s5_scan",
    )(u_flat, mats["m"], mats["wre"], mats["wim"], mats["vre"], mats["vim"], mats["are"], mats["aim"])


GLU_TM = 512


def _glu_kernel(y_ref, u_ref, d_ref, w_ref, o_ref):
    ys = y_ref[...] + d_ref[...] * u_ref[...].astype(F32)
    cdf = 0.5 * (1.0 + jnp.tanh(math.sqrt(2.0 / math.pi) * (ys + 0.044715 * (ys * ys * ys))))
    ys = ys * cdf
    z = _dot(ys.astype(BF16), w_ref[...])
    o_ref[...] = (ys * _sigmoid(z)).astype(o_ref.dtype)


def _glu(y, proj, d, w):
    tm = GLU_TM
    return pl.pallas_call(
        _glu_kernel,
        grid=(SEQ // tm,),
        in_specs=[
            pl.BlockSpec((tm, D_SSM), lambda i: (i, 0)),
            pl.BlockSpec((tm, D_SSM), lambda i: (i, 0)),
            pl.BlockSpec((1, D_SSM), lambda i: (0, 0)),
            pl.BlockSpec((D_SSM, D_SSM), lambda i: (0, 0)),
        ],
        out_specs=pl.BlockSpec((tm, D_SSM), lambda i: (i, 0)),
        out_shape=jax.ShapeDtypeStruct((SEQ, D_SSM), BF16),
        compiler_params=_params("arbitrary"),
        name="glu",
    )(y, proj, d.reshape(1, D_SSM), w)


PREP_TM = 512


def _rope_tables():
    half = ROT_DIM // 2
    pos = jnp.arange(SEQ, dtype=F32)
    inv = jnp.power(ROPE_THETA, -jnp.arange(0, ROT_DIM, 2, dtype=F32) / ROT_DIM)
    ang = pos[:, None] * inv[None, :]
    cos, sin = jnp.cos(ang), jnp.sin(ang)
    one = jnp.ones((SEQ, HEAD_DIM - ROT_DIM), F32)
    zero = jnp.zeros((SEQ, HEAD_DIM - half), F32)
    zero8 = jnp.zeros((SEQ, half), F32)
    c_tab = jnp.concatenate([cos, cos, one], axis=1)
    s_up = jnp.concatenate([-sin, zero], axis=1)
    s_dn = jnp.concatenate([zero8, sin, one * 0.0], axis=1)
    tile2 = lambda a: jnp.concatenate([a, a], axis=1)
    return tile2(c_tab), tile2(s_up), tile2(s_dn)


def _prep_kernel(q_ref, k_ref, v_ref, qg_ref, kg_ref, c_ref, su_ref, sd_ref, ones_ref,
                 qo_ref, ko_ref, vo_ref):
    half = ROT_DIM // 2
    ctab, sup, sdn = c_ref[...], su_ref[...], sd_ref[...]
    seg_ones = ones_ref[...]

    def norm_rope(x_ref, gain, scale, o_ref):
        for h in range(N_HEADS):
            xh = x_ref[:, h * LANES:(h + 1) * LANES].astype(F32)
            sq = xh * xh
            hi = sq.astype(BF16)
            lo = (sq - hi.astype(F32)).astype(BF16)
            seg = _dot(hi, seg_ones) + _dot(lo, seg_ones)
            xn = xh * lax.rsqrt(seg * (1.0 / HEAD_DIM) + EPS) * gain
            r = (xn * ctab + pltpu.roll(xn, LANES - half, 1) * sup + pltpu.roll(xn, half, 1) * sdn) * scale
            o_ref[2 * h] = r[:, :HEAD_DIM].astype(o_ref.dtype)
            o_ref[2 * h + 1] = r[:, HEAD_DIM:].astype(o_ref.dtype)

    norm_rope(q_ref, qg_ref[...], (HEAD_DIM ** -0.5) * LOG2E, qo_ref)
    norm_rope(k_ref, kg_ref[...], 1.0, ko_ref)
    ones = jnp.ones((PREP_TM, V_DIM), BF16)
    for h in range(N_HEADS):
        vo_ref[h, :, :V_DIM] = v_ref[:, h * V_DIM:(h + 1) * V_DIM]
        vo_ref[h, :, V_DIM:] = ones


def _prep(proj, q_g, k_g, tables):
    tm = PREP_TM
    ctab, sup, sdn = tables
    seg = (jnp.arange(LANES)[:, None] // HEAD_DIM == jnp.arange(LANES)[None, :] // HEAD_DIM).astype(BF16)
    gain2 = lambda g: jnp.concatenate([g, g]).reshape(1, LANES).astype(F32)
    tab_spec = pl.BlockSpec((tm, LANES), lambda i: (i, 0))
    row_spec = pl.BlockSpec((1, LANES), lambda i: (0, 0))
    return pl.pallas_call(
        _prep_kernel,
        grid=(SEQ // tm,),
        in_specs=[
            pl.BlockSpec((tm, D_ATTN), lambda i: (i, 1)),
            pl.BlockSpec((tm, D_ATTN), lambda i: (i, 2)),
            pl.BlockSpec((tm, D_ATTN), lambda i: (i, 3)),
            row_spec, row_spec, tab_spec, tab_spec, tab_spec,
            pl.BlockSpec((LANES, LANES), lambda i: (0, 0)),
        ],
        out_specs=[
            pl.BlockSpec((2 * N_HEADS, tm, HEAD_DIM), lambda i: (0, i, 0)),
            pl.BlockSpec((2 * N_HEADS, tm, HEAD_DIM), lambda i: (0, i, 0)),
            pl.BlockSpec((N_HEADS, tm, 2 * V_DIM), lambda i: (0, i, 0)),
        ],
        out_shape=[
            jax.ShapeDtypeStruct((2 * N_HEADS, SEQ, HEAD_DIM), BF16),
            jax.ShapeDtypeStruct((2 * N_HEADS, SEQ, HEAD_DIM), BF16),
            jax.ShapeDtypeStruct((N_HEADS, SEQ, 2 * V_DIM), BF16),
        ],
        compiler_params=_params("arbitrary"),
        name="attn_prep",
    )(proj, proj, proj, gain2(q_g), gain2(k_g), ctab, sup, sdn, seg)


ATT_TQ = 512
ATT_TK = 1024


def _attn_kernel(lam_ref, q1_ref, q2_ref, k1_ref, k2_ref, v_ref, sg_ref, o_ref, acc_ref, m_ref, *, out_scale):
    j = pl.program_id(2)

    @pl.when(j == 0)
    def _():
        acc_ref[...] = jnp.zeros_like(acc_ref)
        m_ref[...] = jnp.full_like(m_ref, -jnp.inf)

    v = v_ref[0]
    for c, (q_ref, k_ref) in enumerate(((q1_ref, k1_ref), (q2_ref, k2_ref))):
        s = lax.dot_general(q_ref[0], k_ref[0], (((1,), (1,)), ((), ())), preferred_element_type=F32)
        m_prev = m_ref[c]
        m_new = jnp.maximum(m_prev, jnp.max(s, axis=1, keepdims=True))
        alpha = jnp.exp2(m_prev - m_new)
        p = jnp.exp2(s - m_new)
        acc_ref[c] = acc_ref[c] * alpha + _dot(p.astype(BF16), v)
        m_ref[c] = m_new

    @pl.when(j == pl.num_programs(2) - 1)
    def _():
        a1, a2 = acc_ref[0], acc_ref[1]
        o = a1[:, :V_DIM] / a1[:, V_DIM:V_DIM + 1] - lam_ref[0] * (a2[:, :V_DIM] / a2[:, V_DIM:V_DIM + 1])
        ms = jnp.mean(o * o, axis=-1, keepdims=True)
        o_ref[...] = (o * lax.rsqrt(ms + EPS) * sg_ref[...] * out_scale).astype(o_ref.dtype)


def _attention(lam, qh, kh, vext, sub_g, lam_init):
    tq, tk = ATT_TQ, ATT_TK
    q_spec = lambda c: pl.BlockSpec((1, tq, HEAD_DIM), lambda h, i, j: (2 * h + c, i, 0))
    k_spec = lambda c: pl.BlockSpec((1, tk, HEAD_DIM), lambda h, i, j: (2 * h + c, j, 0))
    return pl.pallas_call(
        functools.partial(_attn_kernel, out_scale=1.0 - lam_init),
        grid=(N_HEADS, SEQ // tq, SEQ // tk),
        in_specs=[
            pl.BlockSpec(memory_space=pltpu.SMEM),
            q_spec(0), q_spec(1), k_spec(0), k_spec(1),
            pl.BlockSpec((1, tk, 2 * V_DIM), lambda h, i, j: (h, j, 0)),
            pl.BlockSpec((1, V_DIM), lambda h, i, j: (0, 0)),
        ],
        out_specs=pl.BlockSpec((tq, V_DIM), lambda h, i, j: (i, h)),
        out_shape=jax.ShapeDtypeStruct((SEQ, D_ATTN), BF16),
        scratch_shapes=[pltpu.VMEM((2, tq, 2 * V_DIM), F32), pltpu.VMEM((2, tq, 1), F32)],
        compiler_params=_params("arbitrary", "arbitrary", "arbitrary"),
        name="diff_attn",
    )(lam.reshape(1), qh, qh, kh, kh, vext, sub_g.reshape(1, V_DIM).astype(F32))


MERGE_TM = 1024
MERGE_TN = 512


def _merge_kernel(ys_ref, ya_ref, ws_ref, wa_ref, g0_ref, g1_ref, b_ref, o_ref):
    b = b_ref[...]
    gate_s = _sigmoid(g0_ref[...].astype(F32) + b[0:1])
    gate_a = _sigmoid(g1_ref[...].astype(F32) + b[1:2])
    o_ref[...] = (gate_s * _dot(ys_ref[...], ws_ref[...]) + gate_a * _dot(ya_ref[...], wa_ref[...])).astype(o_ref.dtype)


def _merge(ys, ya, w_s, w_a, proj, gate_b):
    tm, tn = MERGE_TM, MERGE_TN
    gl0 = (D_SSM + 3 * D_ATTN) // tn
    gl1 = gl0 + D_MODEL // tn
    return pl.pallas_call(
        _merge_kernel,
        grid=(SEQ // tm, D_MODEL // tn),
        in_specs=[
            pl.BlockSpec((tm, D_SSM), lambda i, j: (i, 0)),
            pl.BlockSpec((tm, D_ATTN), lambda i, j: (i, 0)),
            pl.BlockSpec((D_SSM, tn), lambda i, j: (0, j)),
            pl.BlockSpec((D_ATTN, tn), lambda i, j: (0, j)),
            pl.BlockSpec((tm, tn), lambda i, j: (i, gl0 + j)),
            pl.BlockSpec((tm, tn), lambda i, j: (i, gl1 + j)),
            pl.BlockSpec((2, tn), lambda i, j: (0, j)),
        ],
        out_specs=pl.BlockSpec((tm, tn), lambda i, j: (i, j)),
        out_shape=jax.ShapeDtypeStruct((SEQ, D_MODEL), BF16),
        compiler_params=_params("arbitrary", "arbitrary"),
        name="merge",
    )(ys, ya, w_s, w_a, proj, proj, gate_b.astype(F32))


OUT_TM = 256


def _outproj_kernel(m_ref, w_ref, x_ref, g_ref, rh_ref, rl_ref, x2_ref, xn_ref, aff_ref):
    x2 = x_ref[...] + _dot(m_ref[...], w_ref[...])
    x2_ref[...] = x2
    ms = jnp.mean(x2 * x2, axis=-1, keepdims=True)
    xn = x2 * lax.rsqrt(ms + EPS) * g_ref[...]
    xn_ref[...] = xn
    xh = xn.astype(BF16)
    xl = (xn - xh.astype(F32)).astype(BF16)
    rh = rh_ref[...]
    logits = _dot(xh, rh) + _dot(xl, rh) + _dot(xh, rl_ref[...])
    lane = lax.broadcasted_iota(jnp.int32, logits.shape, 1)
    logits = jnp.where(lane < N_EXPERTS, logits, -jnp.inf)
    e = jnp.exp(logits - jnp.max(logits, axis=-1, keepdims=True))
    aff_ref[...] = e / jnp.sum(e, axis=-1, keepdims=True)


def _outproj(merged, w_out, x, g, w_router):
    tm = OUT_TM
    wr = jnp.zeros((D_MODEL, LANES), F32).at[:, :N_EXPERTS].set(w_router.astype(F32))
    wr_hi = wr.astype(BF16)
    wr_lo = (wr - wr_hi.astype(F32)).astype(BF16)
    row = pl.BlockSpec((tm, D_MODEL), lambda i: (i, 0))
    return pl.pallas_call(
        _outproj_kernel,
        grid=(SEQ // tm,),
        in_specs=[
            row,
            pl.BlockSpec((D_MODEL, D_MODEL), lambda i: (0, 0)),
            row,
            pl.BlockSpec((1, D_MODEL), lambda i: (0, 0)),
            pl.BlockSpec((D_MODEL, LANES), lambda i: (0, 0)),
            pl.BlockSpec((D_MODEL, LANES), lambda i: (0, 0)),
        ],
        out_specs=[row, row, pl.BlockSpec((tm, LANES), lambda i: (i, 0))],
        out_shape=[
            jax.ShapeDtypeStruct((SEQ, D_MODEL), F32),
            jax.ShapeDtypeStruct((SEQ, D_MODEL), F32),
            jax.ShapeDtypeStruct((SEQ, LANES), F32),
        ],
        compiler_params=_params("arbitrary"),
        name="outproj_router",
    )(merged, w_out, x, g.reshape(1, D_MODEL).astype(F32), wr_hi, wr_lo)


def _cumsum_tokens(mask, tri, low):
    local = _dot(mask.astype(BF16), tri)
    totals = jnp.broadcast_to(local[:, LANES - 1:LANES], local.shape)
    return local + _dot(low, totals.astype(BF16))


def _route_kernel(aff_ref, idx_hbm, gate_hbm, selpos_v, selpos_s, aff_s, idx_s, gate_s, sem):
    e = pl.program_id(0)
    a = aff_ref[0]
    bits = pltpu.bitcast(a, jnp.int32)

    def bisect(_, lohi):
        lo, hi = lohi
        mid = lo + (hi - lo + 1) // 2
        cnt = jnp.sum((bits >= mid).astype(jnp.int32))
        ok = cnt >= CAPACITY
        return jnp.where(ok, mid, lo), jnp.where(ok, hi, mid - 1)

    thr, _ = lax.fori_loop(0, 31, bisect, (jnp.int32(0), jnp.int32(0x7F7FFFFF)))

    r = lax.broadcasted_iota(jnp.int32, (LANES, LANES), 0)
    cidx = lax.broadcasted_iota(jnp.int32, (LANES, LANES), 1)
    tri = (r <= cidx).astype(BF16)
    rb = lax.broadcasted_iota(jnp.int32, (TOK_BLOCKS, TOK_BLOCKS), 0)
    cb = lax.broadcasted_iota(jnp.int32, (TOK_BLOCKS, TOK_BLOCKS), 1)
    low = (cb < rb).astype(BF16)

    above = bits > thr
    tie = bits == thr
    need = (CAPACITY - jnp.sum(above.astype(jnp.int32))).astype(F32)
    tie_rank = _cumsum_tokens(tie, tri, low)
    sel = above | (tie & (tie_rank <= need))
    pos = _cumsum_tokens(sel, tri, low).astype(jnp.int32) - 1
    selpos_v[...] = jnp.where(sel, pos, CAPACITY)

    cp_pos = pltpu.make_async_copy(selpos_v, selpos_s, sem.at[0])
    cp_aff = pltpu.make_async_copy(aff_ref.at[0], aff_s, sem.at[1])
    cp_pos.start()
    cp_aff.start()
    cp_pos.wait()
    cp_aff.wait()

    def scatter(t, carry):
        row, col = lax.shift_right_logical(t, 7), t & (LANES - 1)
        slot = selpos_s[row, col]
        srow, scol = lax.shift_right_logical(slot, 7), slot & (LANES - 1)
        idx_s[srow, scol] = t
        gate_s[srow, scol] = aff_s[row, col]
        return carry

    lax.fori_loop(0, SEQ, scatter, 0, unroll=8)

    keep = pl.ds(0, CAPACITY // LANES)
    cp_idx = pltpu.make_async_copy(idx_s.at[keep], idx_hbm.at[e], sem.at[0])
    cp_gate = pltpu.make_async_copy(gate_s.at[keep], gate_hbm.at[e], sem.at[1])
    cp_idx.start()
    cp_gate.start()
    cp_idx.wait()
    cp_gate.wait()


def _route(aff_blocks):
    return pl.pallas_call(
        _route_kernel,
        grid=(N_EXPERTS,),
        in_specs=[pl.BlockSpec((1, TOK_BLOCKS, LANES), lambda e: (e, 0, 0))],
        out_specs=[pl.BlockSpec(memory_space=pl.ANY), pl.BlockSpec(memory_space=pl.ANY)],
        out_shape=[
            jax.ShapeDtypeStruct((N_EXPERTS, CAPACITY // LANES, LANES), jnp.int32),
            jax.ShapeDtypeStruct((N_EXPERTS, CAPACITY // LANES, LANES), F32),
        ],
        scratch_shapes=[
            pltpu.VMEM((TOK_BLOCKS, LANES), jnp.int32),
            pltpu.SMEM((TOK_BLOCKS, LANES), jnp.int32),
            pltpu.SMEM((TOK_BLOCKS, LANES), F32),
            pltpu.SMEM((SLOT_ROWS, LANES), jnp.int32),
            pltpu.SMEM((SLOT_ROWS, LANES), F32),
            pltpu.SemaphoreType.DMA((2,)),
        ],
        compiler_params=_params("arbitrary"),
        name="route",
    )(aff_blocks)


EXP_TF = 256


def _expert_kernel(idx_ref, xn_hbm, acc_in, gate_ref, wg_ref, wu_ref, wd_ref, acc_hbm,
                   stage, xb, ybuf, sem):
    del acc_in
    e = pl.program_id(0)
    f = pl.program_id(1)
    last = pl.num_programs(1) - 1

    def row_copy(src, dst, c, to_hbm, s):
        t = idx_ref[e, c]
        if to_hbm:
            return pltpu.make_async_copy(src.at[pl.ds(c, 1)], dst.at[pl.ds(t, 1)], sem.at[s])
        return pltpu.make_async_copy(src.at[pl.ds(t, 1)], dst.at[pl.ds(c, 1)], sem.at[s])

    def for_rows(fn):
        def body(c, carry):
            fn(c)
            return carry
        lax.fori_loop(0, CAPACITY, body, 0, unroll=8)

    @pl.when(f == 0)
    def _():
        for_rows(lambda c: row_copy(xn_hbm, stage, c, False, 0).start())
        for_rows(lambda c: row_copy(xn_hbm, stage, c, False, 0).wait())
        xb[...] = stage[...].astype(BF16)
        for_rows(lambda c: row_copy(acc_hbm, stage, c, False, 1).start())
        ybuf[...] = jnp.zeros_like(ybuf)

    x = xb[...]
    hg = _dot(x, wg_ref[0].astype(BF16))
    hu = _dot(x, wu_ref[0].astype(BF16))
    h = hg * _sigmoid(hg) * hu * gate_ref[0]
    ybuf[...] += _dot(h.astype(BF16), wd_ref[0].astype(BF16))

    @pl.when(f == last)
    def _():
        for_rows(lambda c: row_copy(acc_hbm, stage, c, False, 1).wait())
        stage[...] += ybuf[...]
        for_rows(lambda c: row_copy(stage, acc_hbm, c, True, 2).start())
        for_rows(lambda c: row_copy(stage, acc_hbm, c, True, 2).wait())


def _experts(idx, gate_col, xn, x2, w_gate, w_up, w_down):
    tf = EXP_TF
    grid_spec = pltpu.PrefetchScalarGridSpec(
        num_scalar_prefetch=1,
        grid=(N_EXPERTS, D_EXPERT // tf),
        in_specs=[
            pl.BlockSpec(memory_space=pl.ANY),
            pl.BlockSpec(memory_space=pl.ANY),
            pl.BlockSpec((1, CAPACITY, 1), lambda e, f, idx: (e, 0, 0)),
            pl.BlockSpec((1, D_MODEL, tf), lambda e, f, idx: (e, 0, f)),
            pl.BlockSpec((1, D_MODEL, tf), lambda e, f, idx: (e, 0, f)),
            pl.BlockSpec((1, tf, D_MODEL), lambda e, f, idx: (e, f, 0)),
        ],
        out_specs=pl.BlockSpec(memory_space=pl.ANY),
        scratch_shapes=[
            pltpu.VMEM((CAPACITY, D_MODEL), F32),
            pltpu.VMEM((CAPACITY, D_MODEL), BF16),
            pltpu.VMEM((CAPACITY, D_MODEL), F32),
            pltpu.SemaphoreType.DMA((3,)),
        ],
    )
    return pl.pallas_call(
        _expert_kernel,
        grid_spec=grid_spec,
        out_shape=jax.ShapeDtypeStruct((SEQ, D_MODEL), F32),
        input_output_aliases={2: 0},
        compiler_params=_params("arbitrary", "arbitrary"),
        name="experts",
    )(idx, xn, x2, gate_col, w_gate, w_up, w_down)


def _lambda_init(layer_idx):
    return 0.8 - 0.6 * math.exp(-0.3 * layer_idx)


def kernel(x, mix_norm_g, w_in, gate_b, ssm_a_re, ssm_a_im, ssm_log_step, ssm_b_re, ssm_b_im, ssm_c_re, ssm_c_im, ssm_d, w_glu, q_norm_g, k_norm_g, lambda_q1, lambda_k1, lambda_q2, lambda_k2, subln_g, w_br_ssm, w_br_attn, w_out, ffn_norm_g, w_router, w_e_gate, w_e_up, w_e_down):
    batch = x.shape[0]
    assert x.shape == (batch, SEQ, D_MODEL) and batch == 1
    tables = _rope_tables()
    xs = x.reshape(SEQ, D_MODEL).astype(F32)
    T, H, G = SSM_CHUNK, SSM_GROUP, N_SSM_GROUPS
    for l in range(DEPTH):
        lam_init = _lambda_init(l)
        proj = _inproj(xs, mix_norm_g[l].astype(F32), w_in[l])

        mats = _s5_matrices(ssm_a_re[l], ssm_a_im[l], ssm_log_step[l], ssm_b_re[l], ssm_b_im[l],
                            ssm_c_re[l], ssm_c_im[l])
        u_flat = proj[:, :D_SSM].reshape(N_CHUNKS, T, G, H).transpose(0, 2, 1, 3).reshape(N_CHUNKS, G * T * H)
        y_flat = _s5(u_flat, mats)
        y_ssm = y_flat.reshape(N_CHUNKS, G, T, H).transpose(0, 2, 1, 3).reshape(SEQ, D_SSM)
        ys = _glu(y_ssm, proj, ssm_d[l].astype(F32), w_glu[l].astype(BF16))

        qh, kh, vext = _prep(proj, q_norm_g[l], k_norm_g[l], tables)
        lam = (jnp.exp(jnp.sum(lambda_q1[l].astype(F32) * lambda_k1[l].astype(F32)))
               - jnp.exp(jnp.sum(lambda_q2[l].astype(F32) * lambda_k2[l].astype(F32))) + lam_init)
        ya = _attention(lam.astype(F32), qh, kh, vext, subln_g[l], lam_init)

        merged = _merge(ys, ya, w_br_ssm[l].astype(BF16), w_br_attn[l].astype(BF16), proj, gate_b[l])
        x2, xn, aff = _outproj(merged, w_out[l].astype(BF16), xs, ffn_norm_g[l], w_router[l])

        aff_blocks = aff[:, :N_EXPERTS].T.reshape(N_EXPERTS, TOK_BLOCKS, LANES)
        idx, gate = _route(aff_blocks)
        idx = idx.reshape(N_EXPERTS, CAPACITY)
        gate_col = gate.reshape(N_EXPERTS, CAPACITY, 1)
        xs = _experts(idx, gate_col, xn, x2, w_e_gate[l], w_e_up[l], w_e_down[l])
    return xs.reshape(batch, SEQ, D_MODEL).astype(x.dtype)
```
